```python
import math
import jax, jax.numpy as jnp
from jax import lax
import numpy as np

D_MODEL = 1024
BATCH = 8
SEQ = 2048
DEPTH = 1

A_HEADS = 8
A_HEAD_DIM = 64
A_WIDTH = A_HEADS * A_HEAD_DIM
IDX_HEADS = 8
IDX_DIM = 32
INDEX_TOPK = 256
SPARSE_Q_BLOCK = 64
N_BUCKETS = 32
MAX_DISTANCE = 128

B_HEADS = 4
B_KEY_DIM = 64
B_VAL_DIM = 128
B_KEY_WIDTH = B_HEADS * B_KEY_DIM
B_WIDTH = B_HEADS * B_VAL_DIM
GATE_RANK = 16
GATE_TEMP = 16.0
GLA_CHUNK = 64

MIX_WIDTH = A_WIDTH + B_WIDTH

PROJ_SPLITS = (A_WIDTH, A_WIDTH, A_WIDTH,
               IDX_HEADS * IDX_DIM, IDX_DIM, IDX_HEADS,
               B_KEY_WIDTH, B_KEY_WIDTH, B_WIDTH,
               GATE_RANK, B_WIDTH)
PROJ_WIDTH = 3 * A_WIDTH + IDX_HEADS * IDX_DIM + IDX_DIM + IDX_HEADS + 2 * B_KEY_WIDTH + B_WIDTH + GATE_RANK + B_WIDTH

N_GROUPS = 4
EXPERTS_PER_GROUP = 8
N_EXPERTS = N_GROUPS * EXPERTS_PER_GROUP
EXPERT_TOPK = 2
D_EXPERT = 256

EPS = 1e-6

kernel_name = "hymba_dsa_gla_hier_moe_layer"


def rmsnorm(x, g):
    xf = x.astype(jnp.float32)
    y = xf * lax.rsqrt(jnp.mean(xf * xf, axis=-1, keepdims=True) + EPS)
    return (y * g.astype(jnp.float32)).astype(x.dtype)


def t5_bucket(dist):
    max_exact = N_BUCKETS // 2
    d_f = jnp.maximum(dist, 1).astype(jnp.float32)
    large = max_exact + (jnp.log(d_f / max_exact) / math.log(MAX_DISTANCE / max_exact)
                         * (N_BUCKETS - max_exact)).astype(jnp.int32)
    large = jnp.minimum(large, N_BUCKETS - 1)
    return jnp.where(dist < max_exact, dist, large)


def dsa_attention(q, k, v, iq, ik, iw, rel_bias):
    bsz, s_len = q.shape[0], q.shape[1]
    topk = min(INDEX_TOPK, s_len // 4)
    nb = s_len // SPARSE_Q_BLOCK

    def blocks(a):
        return a.reshape(bsz, nb, SPARSE_Q_BLOCK, *a.shape[2:]).swapaxes(0, 1)

    q_pos = jnp.arange(s_len, dtype=jnp.int32).reshape(nb, SPARSE_Q_BLOCK)
    key_pos = jnp.arange(s_len, dtype=jnp.int32)
    ik_f = ik.astype(jnp.float32)
    gather = jax.vmap(lambda a, i: a[i])

    def one_block(args):
        qb, iqb, iwb, tb = args
        idx_logits = jnp.einsum('bqhd,bsd->bqhs', iqb.astype(jnp.float32), ik_f) * IDX_DIM ** -0.5
        score = jnp.einsum('bqhs,bqh->bqs', jax.nn.relu(idx_logits),
                           iwb.astype(jnp.float32) * IDX_HEADS ** -0.5)
        causal = key_pos[None, :] <= tb[:, None]
        score = jnp.where(causal[None], score, -jnp.inf)
        _, idx = lax.top_k(score, topk)
        valid = idx <= tb[None, :, None]
        k_sel = gather(k, idx)
        v_sel = gather(v, idx)
        logits = jnp.einsum('bqhd,bqkhd->bhqk', qb, k_sel).astype(jnp.float32) * A_HEAD_DIM ** -0.5
        dist = jnp.maximum(tb[None, :, None] - idx, 0)
        bias = rel_bias[t5_bucket(dist)]
        logits = logits + jnp.transpose(bias, (0, 3, 1, 2)).astype(jnp.float32)
        logits = jnp.where(valid[:, None], logits, -jnp.inf)
        p = jax.nn.softmax(logits, axis=-1).astype(v.dtype)
        return jnp.einsum('bhqk,bqkhd->bqhd', p, v_sel)

    out = lax.map(one_block, (blocks(q), blocks(iq), blocks(iw), q_pos))
    return out.swapaxes(0, 1).reshape(bsz, s_len, A_HEADS * A_HEAD_DIM)


def gla_chunked(q, k, v, log_a):
    bsz, s_len, h, dk = q.shape
    dv = v.shape[-1]
    nc = s_len // GLA_CHUNK

    def chunks(a):
        return a.reshape(bsz, nc, GLA_CHUNK, h, a.shape[-1]).transpose(1, 0, 3, 2, 4)

    tri = jnp.tril(jnp.ones((GLA_CHUNK, GLA_CHUNK), dtype=bool))

    def step(state, inp):
        qc, kc, vc, gc = inp
        b = jnp.cumsum(gc, axis=2)
        o_inter = jnp.einsum('bhcd,bhdv->bhcv', qc * jnp.exp(b), state)
        diff = b[:, :, :, None, :] - b[:, :, None, :, :]
        decay = jnp.exp(jnp.where(tri[None, None, :, :, None], diff, -jnp.inf))
        attn = jnp.einsum('bhtd,bhsd,bhtsd->bhts', qc, kc, decay)
        o_intra = jnp.einsum('bhts,bhsv->bhtv', attn, vc)
        b_last = b[:, :, -1:, :]
        state = (jnp.exp(b_last[:, :, 0, :])[..., None] * state
                 + jnp.einsum('bhsd,bhsv->bhdv', kc * jnp.exp(b_last - b), vc))
        return state, o_inter + o_intra

    init = jnp.zeros((bsz, h, dk, dv), jnp.float32)
    _, o = lax.scan(step, init, (chunks(q * dk ** -0.5), chunks(k), chunks(v), chunks(log_a)))
    return o.transpose(1, 0, 3, 2, 4).reshape(bsz, s_len, h, dv)


def hier_moe(h, w_rg, b_rg, w_re, b_re, w_g, w_u, w_d):
    bsz, s_len, d = h.shape
    hf = h.reshape(-1, d)
    g_logits = (hf @ w_rg).astype(jnp.float32) + b_rg.astype(jnp.float32)
    g_prob = jax.nn.softmax(g_logits, axis=-1)
    _, g_sel = lax.top_k(g_logits, 1)
    g_onehot = jax.nn.one_hot(g_sel[:, 0], N_GROUPS, dtype=jnp.float32)
    g_weight = jnp.sum(g_prob * g_onehot, axis=-1, keepdims=True)
    e_logits = ((hf @ w_re).astype(jnp.float32) + b_re.astype(jnp.float32)).reshape(-1, N_GROUPS, EXPERTS_PER_GROUP)
    e_logits_sel = jnp.einsum('nge,ng->ne', e_logits, g_onehot)
    top_val, top_idx = lax.top_k(e_logits_sel, EXPERT_TOPK)
    e_weight = jax.nn.softmax(top_val, axis=-1) * g_weight
    e_idx = g_sel * EXPERTS_PER_GROUP + top_idx
    gates = jnp.sum(jax.nn.one_hot(e_idx, N_EXPERTS, dtype=jnp.float32) * e_weight[..., None], axis=1)
    out = jnp.zeros(hf.shape, jnp.float32)
    for g in range(N_GROUPS):
        sl = slice(g * EXPERTS_PER_GROUP, (g + 1) * EXPERTS_PER_GROUP)
        a = jnp.einsum('nd,edf->nef', hf, w_g[sl])
        u = jnp.einsum('nd,edf->nef', hf, w_u[sl])
        hid = jax.nn.silu(a) * u * gates[:, sl, None].astype(hf.dtype)
        out = out + jnp.einsum('nef,efd->nd', hid, w_d[sl]).astype(jnp.float32)
    return out.astype(h.dtype).reshape(bsz, s_len, d)


def setup_inputs(seed: int = 0) -> dict:
    key = jax.random.key(seed)
    ks = jax.random.split(key, 20)
    f32 = jnp.float32
    nrm = lambda k, shape, scale: jax.random.normal(k, shape, f32) * scale
    return {
        "x": nrm(ks[0], (BATCH, SEQ, D_MODEL), 1.0),
        "norm1_g": 1.0 + nrm(ks[1], (DEPTH, D_MODEL), 0.01),
        "w_in": nrm(ks[2], (DEPTH, D_MODEL, PROJ_WIDTH), D_MODEL ** -0.5),
        "q_norm_g": 1.0 + nrm(ks[3], (DEPTH, A_HEAD_DIM), 0.01),
        "k_norm_g": 1.0 + nrm(ks[4], (DEPTH, A_HEAD_DIM), 0.01),
        "rel_bias": nrm(ks[5], (N_BUCKETS, A_HEADS), 0.5),
        "gla_gate_w2": nrm(ks[6], (DEPTH, GATE_RANK, B_KEY_WIDTH), GATE_RANK ** -0.5),
        "gla_gate_b": nrm(ks[7], (DEPTH, B_KEY_WIDTH), 0.1),
        "gla_out_norm_g": 1.0 + nrm(ks[8], (DEPTH, B_VAL_DIM), 0.01),
        "w_out": nrm(ks[9], (DEPTH, MIX_WIDTH, D_MODEL), MIX_WIDTH ** -0.5),
        "norm2_g": 1.0 + nrm(ks[10], (DEPTH, D_MODEL), 0.01),
        "w_router_group": nrm(ks[11], (DEPTH, D_MODEL, N_GROUPS), D_MODEL ** -0.5),
        "b_router_group": nrm(ks[12], (DEPTH, N_GROUPS), 0.01),
        "w_router_expert": nrm(ks[13], (DEPTH, D_MODEL, N_EXPERTS), D_MODEL ** -0.5),
        "b_router_expert": nrm(ks[14], (DEPTH, N_EXPERTS), 0.01),
        "w_exp_gate": nrm(ks[15], (DEPTH, N_EXPERTS, D_MODEL, D_EXPERT), D_MODEL ** -0.5),
        "w_exp_up": nrm(ks[16], (DEPTH, N_EXPERTS, D_MODEL, D_EXPERT), D_MODEL ** -0.5),
        "w_exp_down": nrm(ks[17], (DEPTH, N_EXPERTS, D_EXPERT, D_MODEL), D_EXPERT ** -0.5),
    }


def reference(x, norm1_g, w_in, q_norm_g, k_norm_g, rel_bias, gla_gate_w2, gla_gate_b,
              gla_out_norm_g, w_out, norm2_g, w_router_group, b_router_group,
              w_router_expert, b_router_expert, w_exp_gate, w_exp_up, w_exp_down):
    bsz, s_len, _ = x.shape
    split_points = np.cumsum(PROJ_SPLITS)[:-1].tolist()
    for l in range(DEPTH):
        h = rmsnorm(x, norm1_g[l])
        proj = h @ w_in[l]
        qa, ka, va, iq, ik, iw, qb, kb, vb, g_lr, r_gate = jnp.split(proj, split_points, axis=-1)

        qa = rmsnorm(qa.reshape(bsz, s_len, A_HEADS, A_HEAD_DIM), q_norm_g[l])
        ka = rmsnorm(ka.reshape(bsz, s_len, A_HEADS, A_HEAD_DIM), k_norm_g[l])
        va = va.reshape(bsz, s_len, A_HEADS, A_HEAD_DIM)
        o_a = dsa_attention(qa, ka, va, iq.reshape(bsz, s_len, IDX_HEADS, IDX_DIM), ik, iw, rel_bias)

        log_a = jax.nn.log_sigmoid((g_lr @ gla_gate_w2[l]).astype(jnp.float32)
                                   + gla_gate_b[l].astype(jnp.float32)) / GATE_TEMP
        o_b = gla_chunked(qb.reshape(bsz, s_len, B_HEADS, B_KEY_DIM).astype(jnp.float32),
                          kb.reshape(bsz, s_len, B_HEADS, B_KEY_DIM).astype(jnp.float32),
                          vb.reshape(bsz, s_len, B_HEADS, B_VAL_DIM).astype(jnp.float32),
                          log_a.reshape(bsz, s_len, B_HEADS, B_KEY_DIM))
        o_b = rmsnorm(o_b, gla_out_norm_g[l]).reshape(bsz, s_len, B_WIDTH).astype(x.dtype) * jax.nn.silu(r_gate)

        x = x + jnp.concatenate([o_a, o_b], axis=-1) @ w_out[l]

        x = x + hier_moe(rmsnorm(x, norm2_g[l]), w_router_group[l], b_router_group[l],
                         w_router_expert[l], b_router_expert[l],
                         w_exp_gate[l], w_exp_up[l], w_exp_down[l])
    return x
```

```python
import functools
import math

import jax
import jax.numpy as jnp
from jax import lax
from jax.experimental import pallas as pl
from jax.experimental.pallas import tpu as pltpu

F32 = jnp.float32
BF16 = jnp.bfloat16
I32 = jnp.int32

A_HEADS = 8
A_HEAD_DIM = 64
A_WIDTH = A_HEADS * A_HEAD_DIM
IDX_HEADS = 8
IDX_DIM = 32
IDX_WIDTH = IDX_HEADS * IDX_DIM
INDEX_TOPK = 256
N_BUCKETS = 32
MAX_DISTANCE = 128
B_HEADS = 4
B_KEY_DIM = 64
B_VAL_DIM = 128
B_WIDTH = B_HEADS * B_VAL_DIM
GATE_RANK = 16
GATE_TEMP = 16.0
GLA_CHUNK = 64
GLA_SUB = 16
N_GROUPS = 4
EXPERTS_PER_GROUP = 8
N_EXPERTS = N_GROUPS * EXPERTS_PER_GROUP
D_EXPERT = 256
EPS = 1e-6

LANES = 128
B_KEY_PAD = LANES
VMEM_LIMIT = 48 * 1024 * 1024
NEG_BIG = -1e30
INT_MIN = -2 ** 31

_C_QA = 0
_C_KA = _C_QA + A_WIDTH
_C_VA = _C_KA + A_WIDTH
_C_IQ = _C_VA + A_WIDTH
_C_IK = _C_IQ + IDX_WIDTH
_C_QB = _C_IK + IDX_WIDTH
_C_KB = _C_QB + B_HEADS * B_KEY_PAD
_C_VB = _C_KB + B_HEADS * B_KEY_PAD
_C_RG = _C_VB + B_WIDTH
_C_SM = _C_RG + B_WIDTH
_C_END = _C_SM + LANES
_SM_IW = 0
_SM_GLR = IDX_HEADS


def _dot(a, b):
    return jnp.dot(a, b, preferred_element_type=F32)


def _dot_nt(a, b):
    return lax.dot_general(a, b, (((1,), (1,)), ((), ())), preferred_element_type=F32)


def _dot_tn(a, b):
    return lax.dot_general(a, b, (((0,), (0,)), ((), ())), preferred_element_type=F32)


def _split_bf16(x):
    hi = x.astype(BF16)
    lo = (x - hi.astype(F32)).astype(BF16)
    return hi, lo


def _bias_kernel(rb_ref, out_ref, *, tq, tk):
    o = pl.program_id(0)
    h = pl.program_id(1)
    row = lax.broadcasted_iota(I32, (tq, tk), 0)
    col = lax.broadcasted_iota(I32, (tq, tk), 1)
    dist = jnp.maximum(o * tq + row - col, 0)
    max_exact = N_BUCKETS // 2
    d_f = jnp.maximum(dist, 1).astype(F32)
    large = max_exact + (jnp.log(d_f / max_exact) / math.log(MAX_DISTANCE / max_exact)
                         * (N_BUCKETS - max_exact)).astype(I32)
    large = jnp.minimum(large, N_BUCKETS - 1)
    bucket = jnp.where(dist < max_exact, dist, large)
    val = jnp.zeros((tq, tk), F32)
    for k in range(N_BUCKETS):
        val = jnp.where(bucket == k, rb_ref[k, h], val)
    out_ref[0, 0] = val


def _bias_tiles(rel_bias, tq, tk):
    assert 2 * tq - tk + 1 >= MAX_DISTANCE
    return pl.pallas_call(
        functools.partial(_bias_kernel, tq=tq, tk=tk),
        grid=(3, A_HEADS),
        in_specs=[pl.BlockSpec(memory_space=pltpu.SMEM)],
        out_specs=pl.BlockSpec((1, 1, tq, tk), lambda o, h: (o, h, 0, 0)),
        out_shape=jax.ShapeDtypeStruct((3, A_HEADS, tq, tk), F32),
        name="bias_tiles",
    )(rel_bias)


def _proj_kernel(x_ref, g1_ref, w_ref, bd_ref, qg_ref, kg_ref, w2_ref, gb_ref,
                 qa_ref, ka_ref, va_ref, iq_ref, ik_ref, sm_ref,
                 qb_ref, kb_ref, vb_ref, la_ref, rg_ref):
    x = x_ref[...]
    ms = jnp.mean(x * x, axis=-1, keepdims=True)
    h = (x * lax.rsqrt(ms + EPS) * g1_ref[...]).astype(BF16)

    def seg(a, b):
        return _dot(h, w_ref[:, a:b])

    def head_norm(v, g_ref, scale):
        hi, lo = _split_bf16(v * v)
        msq = _dot(hi, bd_ref[...]) + _dot(lo, bd_ref[...])
        return v * lax.rsqrt(msq + EPS) * (g_ref[...] * scale)

    qa_ref[...] = head_norm(seg(_C_QA, _C_KA), qg_ref, A_HEAD_DIM ** -0.5).astype(BF16)
    ka_ref[...] = head_norm(seg(_C_KA, _C_VA), kg_ref, 1.0).astype(BF16)
    va_ref[...] = seg(_C_VA, _C_IQ).astype(BF16)
    iq_ref[...] = (seg(_C_IQ, _C_IK) * IDX_DIM ** -0.5).astype(BF16)
    ik_ref[...] = seg(_C_IK, _C_QB).astype(BF16)
    qb_ref[...] = seg(_C_QB, _C_KB) * B_KEY_DIM ** -0.5
    kb_ref[...] = seg(_C_KB, _C_VB)
    vb_ref[...] = seg(_C_VB, _C_RG).astype(BF16)
    r = seg(_C_RG, _C_SM)
    rg_ref[...] = (r / (1.0 + jnp.exp(-r))).astype(BF16)
    sm = seg(_C_SM, _C_END)
    sm_ref[...] = sm
    z = _dot(sm.astype(BF16), w2_ref[...]) + gb_ref[...]
    log_sig = jnp.minimum(z, 0.0) - jnp.log(1.0 + jnp.exp(-jnp.abs(z)))
    la_ref[...] = log_sig / GATE_TEMP


def _proj(x2, g1, w_all, bd, qg, kg, w2p, gbp, tm):
    n, d = x2.shape
    kp = B_HEADS * B_KEY_PAD

    def rows(width):
        return pl.BlockSpec((tm, width), lambda i: (i, 0))

    def full(shape):
        return pl.BlockSpec(shape, lambda i: (0, 0))

    outs = [
        (A_WIDTH, BF16), (A_WIDTH, BF16), (A_WIDTH, BF16),
        (IDX_WIDTH, BF16), (IDX_WIDTH, BF16), (LANES, F32),
        (kp, F32), (kp, F32), (B_WIDTH, BF16),
        (kp, F32), (B_WIDTH, BF16),
    ]
    return pl.pallas_call(
        _proj_kernel,
        grid=(n // tm,),
        in_specs=[rows(d), full((1, d)), full(w_all.shape), full(bd.shape),
                  full(qg.shape), full(kg.shape), full(w2p.shape), full(gbp.shape)],
        out_specs=[rows(w) for w, _ in outs],
        out_shape=[jax.ShapeDtypeStruct((n, w), dt) for w, dt in outs],
        compiler_params=pltpu.CompilerParams(
            dimension_semantics=("parallel",), vmem_limit_bytes=VMEM_LIMIT),
        name="proj",
    )(x2, g1, w_all, bd, qg, kg, w2p, gbp)


def _select_kernel(iq_ref, sm_ref, ik_ref, mask_ref, keys_scr, iqm_scr, w_scr,
                   t_scr, cge_scr, cnt_scr, *, bh, tq, tk, nk, topk, s_len):
    qi = pl.program_id(1)
    n_live = qi + 1
    nc = tk // LANES
    lane_iq = lax.broadcasted_iota(I32, (tq, IDX_WIDTH), 1)

    for b in range(bh):
        iq = iq_ref[b]
        for h in range(IDX_HEADS):
            in_head = (lane_iq >= h * IDX_DIM) & (lane_iq < (h + 1) * IDX_DIM)
            iqm_scr[b, h] = jnp.where(in_head, iq, jnp.zeros_like(iq))
            wcol = sm_ref[b, :, _SM_IW + h:_SM_IW + h + 1] * IDX_HEADS ** -0.5
            w_scr[b, h] = jnp.broadcast_to(wcol, (tq, LANES))

    row_t = qi * tq + lax.broadcasted_iota(I32, (tq, LANES), 0)
    lane = lax.broadcasted_iota(I32, (tq, LANES), 1)

    def score_tile(kt, carry):
        koff = pl.multiple_of(kt * tk, tk)
        for b in range(bh):
            ik = ik_ref[b, pl.ds(koff, tk), :]
            acc = [jnp.zeros((tq, LANES), F32) for _ in range(nc)]
            for h in range(IDX_HEADS):
                logit = _dot_nt(iqm_scr[b, h], ik)
                wb = w_scr[b, h]
                for c in range(nc):
                    acc[c] = acc[c] + jnp.maximum(logit[:, c * LANES:(c + 1) * LANES], 0.0) * wb
            for c in range(nc):
                bits = pltpu.bitcast(acc[c], I32)
                key = bits ^ ((bits >> 31) & 0x7FFFFFFF)
                col_s = kt * tk + c * LANES + lane
                key = jnp.where(col_s <= row_t, key, INT_MIN)
                keys_scr[b, kt, :, c * LANES:(c + 1) * LANES] = key
        return carry

    lax.fori_loop(0, n_live, score_tile, 0)

    def count_ge(b, cand):
        def body(kt, cnt):
            for c in range(nc):
                k = keys_scr[b, kt, :, c * LANES:(c + 1) * LANES]
                cnt = cnt + jnp.where(k >= cand, 1, 0)
            return cnt
        cnt = lax.fori_loop(0, n_live, body, jnp.zeros((tq, LANES), I32))
        return cnt

    def lane_total(cnt):
        tot = jnp.sum(cnt, axis=-1, keepdims=True)
        return jnp.broadcast_to(tot, (tq, LANES))

    for b in range(bh):
        t_scr[b] = jnp.full((tq, LANES), INT_MIN, I32)
        cge_scr[b] = jnp.full((tq, LANES), s_len, I32)

    def bisect(i, carry):
        bit = jnp.left_shift(jnp.int32(1), 31 - i)
        for b in range(bh):
            cnt_scr[b] = count_ge(b, t_scr[b] + bit)
        for b in range(bh):
            tot = lane_total(cnt_scr[b])
            ok = tot >= topk
            t_scr[b] = jnp.where(ok, t_scr[b] + bit, t_scr[b])
            cge_scr[b] = jnp.where(ok, tot, cge_scr[b])
        return carry

    lax.fori_loop(0, 32, bisect, 0)

    for b in range(bh):
        thr = t_scr[b]

        def gt_body(kt, cnt):
            for c in range(nc):
                k = keys_scr[b, kt, :, c * LANES:(c + 1) * LANES]
                cnt = cnt + jnp.where(k > thr, 1, 0)
            return cnt

        cnt_gt = lane_total(lax.fori_loop(0, n_live, gt_body, jnp.zeros((tq, LANES), I32)))
        room = topk - cnt_gt
        real = thr > INT_MIN
        tied = real & (cge_scr[b] > topk)
        any_tied = jnp.max(jnp.where(tied, 1, 0)) > 0
        cnt_scr[b] = jnp.where(real, jnp.full((tq, LANES), 2 * s_len, I32), 0)

        @pl.when(any_tied)
        def _():
            def cut_bit(j, cut):
                trial = cut + jnp.left_shift(jnp.int32(1), (s_len.bit_length() - 1) - j)

                def eq_body(kt, cnt):
                    for c in range(nc):
                        k = keys_scr[b, kt, :, c * LANES:(c + 1) * LANES]
                        col_s = kt * tk + c * LANES + lane
                        cnt = cnt + jnp.where((k == thr) & (col_s < trial), 1, 0)
                    return cnt

                n_eq = lane_total(lax.fori_loop(0, n_live, eq_body,
                                                jnp.zeros((tq, LANES), I32)))
                return jnp.where(n_eq <= room, trial, cut)

            cut = lax.fori_loop(0, s_len.bit_length(), cut_bit, jnp.zeros((tq, LANES), I32))
            cnt_scr[b] = jnp.where(real, cut, 0)

    def emit(kt, carry):
        for b in range(bh):
            thr = t_scr[b]
            cut = cnt_scr[b]
            for c in range(nc):
                k = keys_scr[b, kt, :, c * LANES:(c + 1) * LANES]
                col_s = kt * tk + c * LANES + lane
                sel = (k > thr) | ((k == thr) & (col_s < cut))
                mask_ref[b, 0, kt, :, c * LANES:(c + 1) * LANES] = jnp.where(
                    sel, 0.0, NEG_BIG).astype(BF16)
        return carry

    lax.fori_loop(0, n_live, emit, 0)

    def emit_dead(kt, carry):
        for b in range(bh):
            mask_ref[b, 0, kt] = jnp.full((tq, tk), NEG_BIG, BF16)
        return carry

    lax.fori_loop(n_live, nk, emit_dead, 0)


def _select(iq, sm, ik, bh, tq, tk, topk):
    bsz, s_len, _ = iq.shape
    nq, nk = s_len // tq, s_len // tk
    kern = functools.partial(_select_kernel, bh=bh, tq=tq, tk=tk, nk=nk, topk=topk,
                             s_len=s_len)
    return pl.pallas_call(
        kern,
        grid=(bsz // bh, nq),
        in_specs=[
            pl.BlockSpec((bh, tq, IDX_WIDTH), lambda g, q: (g, q, 0)),
            pl.BlockSpec((bh, tq, LANES), lambda g, q: (g, q, 0)),
            pl.BlockSpec((bh, s_len, IDX_WIDTH), lambda g, q: (g, 0, 0)),
        ],
        out_specs=pl.BlockSpec((bh, 1, nk, tq, tk), lambda g, q: (g, q, 0, 0, 0)),
        out_shape=jax.ShapeDtypeStruct((bsz, nq, nk, tq, tk), BF16),
        scratch_shapes=[
            pltpu.VMEM((bh, nk, tq, tk), I32),
            pltpu.VMEM((bh, IDX_HEADS, tq, IDX_WIDTH), BF16),
            pltpu.VMEM((bh, IDX_HEADS, tq, LANES), F32),
            pltpu.VMEM((bh, tq, LANES), I32),
            pltpu.VMEM((bh, tq, LANES), I32),
            pltpu.VMEM((bh, tq, LANES), I32),
        ],
        compiler_params=pltpu.CompilerParams(
            dimension_semantics=("parallel", "arbitrary"), vmem_limit_bytes=VMEM_LIMIT),
        name="select",
    )(iq, sm, ik)


def _attn_kernel(q_ref, k_ref, v_ref, mask_ref, bias_ref, o_ref, m_scr, l_scr, acc_scr,
                 *, tq, tk):
    qi = pl.program_id(1)
    dh = A_HEAD_DIM
    m_scr[...] = jnp.full(m_scr.shape, -jnp.inf, F32)
    l_scr[...] = jnp.zeros(l_scr.shape, F32)
    acc_scr[...] = jnp.zeros(acc_scr.shape, F32)
    reps = tk // LANES

    def body(kt, carry):
        koff = pl.multiple_of(kt * tk, tk)
        mk = mask_ref[0, 0, kt].astype(F32)
        bidx = jnp.minimum(qi - kt, 2)
        for h in range(A_HEADS):
            qh = q_ref[0, :, h * dh:(h + 1) * dh]
            kh = k_ref[0, pl.ds(koff, tk), h * dh:(h + 1) * dh]
            vh = v_ref[0, pl.ds(koff, tk), h * dh:(h + 1) * dh]
            s = _dot_nt(qh, kh) + bias_ref[bidx, h] + mk
            m_prev = m_scr[h]
            m_new = jnp.maximum(m_prev, jnp.max(s, axis=-1, keepdims=True))
            alpha = jnp.exp(m_prev - m_new)
            p = jnp.exp(s - jnp.concatenate([m_new] * reps, axis=1))
            l_scr[h] = alpha * l_scr[h] + jnp.sum(p, axis=-1, keepdims=True)
            acc_scr[h] = acc_scr[h] * alpha[:, :dh] + _dot(p.astype(BF16), vh)
            m_scr[h] = m_new
        return carry

    lax.fori_loop(0, qi + 1, body, 0)
    outs = [acc_scr[h] / l_scr[h][:, :dh] for h in range(A_HEADS)]
    o_ref[0] = jnp.concatenate(outs, axis=1).astype(BF16)


def _attention(qa, ka, va, mask, bias, tq, tk):
    bsz, s_len, w = qa.shape
    nq, nk = s_len // tq, s_len // tk
    return pl.pallas_call(
        functools.partial(_attn_kernel, tq=tq, tk=tk),
        grid=(bsz, nq),
        in_specs=[
            pl.BlockSpec((1, tq, w), lambda b, q: (b, q, 0)),
            pl.BlockSpec((1, s_len, w), lambda b, q: (b, 0, 0)),
            pl.BlockSpec((1, s_len, w), lambda b, q: (b, 0, 0)),
            pl.BlockSpec((1, 1, nk, tq, tk), lambda b, q: (b, q, 0, 0, 0)),
            pl.BlockSpec(bias.shape, lambda b, q: (0, 0, 0, 0)),
        ],
        out_specs=pl.BlockSpec((1, tq, w), lambda b, q: (b, q, 0)),
        out_shape=jax.ShapeDtypeStruct((bsz, s_len, w), BF16),
        scratch_shapes=[
            pltpu.VMEM((A_HEADS, tq, LANES), F32),
            pltpu.VMEM((A_HEADS, tq, LANES), F32),
            pltpu.VMEM((A_HEADS, tq, A_HEAD_DIM), F32),
        ],
        compiler_params=pltpu.CompilerParams(
            dimension_semantics=("parallel", "arbitrary"), vmem_limit_bytes=VMEM_LIMIT),
        name="attention",
    )(qa, ka, va, mask, bias)


def _gla_kernel(q_ref, k_ref, v_ref, g_ref, rg_ref, gn_ref, o_ref, state_scr, *, s_len):
    c_len, sub = GLA_CHUNK, GLA_SUB
    state_scr[...] = jnp.zeros(state_scr.shape, F32)
    tri = (lax.broadcasted_iota(I32, (c_len, c_len), 1)
           <= lax.broadcasted_iota(I32, (c_len, c_len), 0)).astype(BF16)

    def chunk(ci, carry):
        r0 = pl.multiple_of(ci * c_len, c_len)
        q = q_ref[0, pl.ds(r0, c_len), :]
        k = k_ref[0, pl.ds(r0, c_len), :]
        v = v_ref[0, pl.ds(r0, c_len), :]
        g_hi, g_lo = _split_bf16(g_ref[0, pl.ds(r0, c_len), :])
        b = _dot(tri, g_hi) + _dot(tri, g_lo)
        b_last = b[c_len - 1:c_len, :]
        state = state_scr[...]
        o = _dot_nt((q * jnp.exp(b)).astype(BF16), state.astype(BF16))
        parts = []
        for i in range(c_len // sub):
            lo, n = i * sub, (i + 1) * sub
            b_ref = b[lo:lo + 1, :]
            qi_ = (q[lo:n] * jnp.exp(b[lo:n] - b_ref)).astype(BF16)
            ki_ = (k[0:n] * jnp.exp(b_ref - b[0:n])).astype(BF16)
            a = _dot_nt(qi_, ki_)
            rr = lo + lax.broadcasted_iota(I32, (sub, n), 0)
            cc = lax.broadcasted_iota(I32, (sub, n), 1)
            a = jnp.where(cc <= rr, a, 0.0)
            parts.append(_dot(a.astype(BF16), v[0:n]))
        o = o + jnp.concatenate(parts, axis=0)
        k_dec = (k * jnp.exp(b_last - b)).astype(BF16)
        state_scr[...] = state * jnp.exp(b_last) + _dot_tn(v, k_dec)
        ms = jnp.mean(o * o, axis=-1, keepdims=True)
        y = o * lax.rsqrt(ms + EPS) * gn_ref[...]
        o_ref[0, pl.ds(r0, c_len), :] = (
            y * rg_ref[0, pl.ds(r0, c_len), :].astype(F32)).astype(BF16)
        return carry

    lax.fori_loop(0, s_len // c_len, chunk, 0)


def _gla(qb, kb, vb, la, rg, gn):
    bsz, s_len, _ = qb.shape

    def blk():
        return pl.BlockSpec((1, s_len, LANES), lambda b, h: (b, 0, h))

    return pl.pallas_call(
        functools.partial(_gla_kernel, s_len=s_len),
        grid=(bsz, B_HEADS),
        in_specs=[blk(), blk(), blk(), blk(), blk(),
                  pl.BlockSpec((1, B_VAL_DIM), lambda b, h: (0, 0))],
        out_specs=blk(),
        out_shape=jax.ShapeDtypeStruct((bsz, s_len, B_WIDTH), BF16),
        scratch_shapes=[pltpu.VMEM((B_VAL_DIM, B_KEY_PAD), F32)],
        compiler_params=pltpu.CompilerParams(
            dimension_semantics=("parallel", "parallel"), vmem_limit_bytes=VMEM_LIMIT),
        name="gla",
    )(qb, kb, vb, la, rg, gn)


def _outproj_kernel(x_ref, oa_ref, ob_ref, wa_ref, wb_ref, g2_ref, wrh_ref, wrl_ref, br_ref,
                    x1_ref, h2_ref, gate_ref):
    x1 = x_ref[...] + _dot(oa_ref[...], wa_ref[...]) + _dot(ob_ref[...], wb_ref[...])
    x1_ref[...] = x1
    ms = jnp.mean(x1 * x1, axis=-1, keepdims=True)
    h2 = x1 * lax.rsqrt(ms + EPS) * g2_ref[...]
    hi, lo = _split_bf16(h2)
    h2_ref[...] = hi
    logits = (_dot(hi, wrh_ref[...]) + _dot(lo, wrh_ref[...]) + _dot(hi, wrl_ref[...])
              + br_ref[...])
    tm = logits.shape[0]
    lane = lax.broadcasted_iota(I32, (tm, LANES), 1).astype(F32)
    far = float(LANES)
    is_g = (lane >= N_EXPERTS) & (lane < N_EXPERTS + N_GROUPS)
    gl = jnp.where(is_g, logits, -jnp.inf)
    gmax = jnp.max(gl, axis=-1, keepdims=True)
    gsum = jnp.sum(jnp.where(is_g, jnp.exp(gl - gmax), 0.0), axis=-1, keepdims=True)
    g_weight = 1.0 / gsum
    g_sel = jnp.min(jnp.where(gl == gmax, lane, far), axis=-1, keepdims=True) - N_EXPERTS
    e_lo = g_sel * EXPERTS_PER_GROUP
    in_grp = (lane >= e_lo) & (lane < e_lo + EXPERTS_PER_GROUP)
    el = jnp.where(in_grp, logits, -jnp.inf)
    v1 = jnp.max(el, axis=-1, keepdims=True)
    i1 = jnp.min(jnp.where(el == v1, lane, far), axis=-1, keepdims=True)
    el2 = jnp.where(lane == i1, -jnp.inf, el)
    v2 = jnp.max(el2, axis=-1, keepdims=True)
    i2 = jnp.min(jnp.where(el2 == v2, lane, far), axis=-1, keepdims=True)
    e21 = jnp.exp(v2 - v1)
    w1 = 1.0 / (1.0 + e21)
    w2 = e21 * w1
    gate_ref[...] = jnp.where(lane == i1, w1 * g_weight,
                              jnp.where(lane == i2, w2 * g_weight, 0.0))


def _outproj(x2, oa, ob, wa, wb, g2, wrh, wrl, br, tm):
    n, d = x2.shape

    def rows(width):
        return pl.BlockSpec((tm, width), lambda i: (i, 0))

    def full(shape):
        return pl.BlockSpec(shape, lambda i: (0, 0))

    return pl.pallas_call(
        _outproj_kernel,
        grid=(n // tm,),
        in_specs=[rows(d), rows(A_WIDTH), rows(B_WIDTH), full(wa.shape), full(wb.shape),
                  full(g2.shape), full(wrh.shape), full(wrl.shape), full(br.shape)],
        out_specs=[rows(d), rows(d), rows(LANES)],
        out_shape=[jax.ShapeDtypeStruct((n, d), F32), jax.ShapeDtypeStruct((n, d), BF16),
                   jax.ShapeDtypeStruct((n, LANES), F32)],
        compiler_params=pltpu.CompilerParams(
            dimension_semantics=("parallel",), vmem_limit_bytes=VMEM_LIMIT),
        name="outproj",
    )(x2, oa, ob, wa, wb, g2, wrh, wrl, br)


def _moe_kernel(h_ref, gate_ref, x1_ref, wg_ref, wu_ref, wd_ref, o_ref):
    e = pl.program_id(1)

    @pl.when(e == 0)
    def _():
        o_ref[...] = x1_ref[...]

    h = h_ref[...]
    a = _dot(h, wg_ref[0])
    u = _dot(h, wu_ref[0])
    lane = lax.broadcasted_iota(I32, gate_ref.shape, 1)
    gcol = jnp.sum(jnp.where(lane == e, gate_ref[...], 0.0), axis=-1, keepdims=True)
    hid = (a / (1.0 + jnp.exp(-a))) * u * gcol
    o_ref[...] += _dot(hid.astype(BF16), wd_ref[0])


def _moe(h2, gates, x1, wg, wu, wd, tm):
    n, d = h2.shape
    ne, _, f = wg.shape
    return pl.pallas_call(
        _moe_kernel,
        grid=(n // tm, ne),
        in_specs=[
            pl.BlockSpec((tm, d), lambda i, e: (i, 0)),
            pl.BlockSpec((tm, LANES), lambda i, e: (i, 0)),
            pl.BlockSpec((tm, d), lambda i, e: (i, 0)),
            pl.BlockSpec((1, d, f), lambda i, e: (e, 0, 0)),
            pl.BlockSpec((1, d, f), lambda i, e: (e, 0, 0)),
            pl.BlockSpec((1, f, d), lambda i, e: (e, 0, 0)),
        ],
        out_specs=pl.BlockSpec((tm, d), lambda i, e: (i, 0)),
        out_shape=jax.ShapeDtypeStruct((n, d), F32),
        compiler_params=pltpu.CompilerParams(
            dimension_semantics=("parallel", "arbitrary"), vmem_limit_bytes=VMEM_LIMIT),
        name="moe",
    )(h2, gates, x1, wg, wu, wd)


def _pad_heads(w, n_heads, dim, pad):
    lead = w.shape[:-1]
    w = w.reshape(*lead, n_heads, dim)
    w = jnp.pad(w, [(0, 0)] * len(lead) + [(0, 0), (0, pad - dim)])
    return w.reshape(*lead, n_heads * pad)


def _pack_w_in(w_in):
    d = w_in.shape[0]
    sizes = (A_WIDTH, A_WIDTH, A_WIDTH, IDX_WIDTH, IDX_DIM, IDX_HEADS,
             B_HEADS * B_KEY_DIM, B_HEADS * B_KEY_DIM, B_WIDTH, GATE_RANK, B_WIDTH)
    offs = [0]
    for sz in sizes:
        offs.append(offs[-1] + sz)
    qa, ka, va, iq, ik, iw, qb, kb, vb, glr, rg = [
        w_in[:, offs[i]:offs[i + 1]] for i in range(len(sizes))]
    small = jnp.concatenate(
        [iw, glr, jnp.zeros((d, LANES - IDX_HEADS - GATE_RANK), w_in.dtype)], axis=1)
    parts = [qa, ka, va, iq, jnp.tile(ik, (1, IDX_HEADS)),
             _pad_heads(qb, B_HEADS, B_KEY_DIM, B_KEY_PAD),
             _pad_heads(kb, B_HEADS, B_KEY_DIM, B_KEY_PAD), vb, rg, small]
    w_all = jnp.concatenate(parts, axis=1).astype(BF16)
    assert w_all.shape[1] == _C_END
    return w_all


def kernel(x, norm1_g, w_in, q_norm_g, k_norm_g, rel_bias, gla_gate_w2, gla_gate_b, gla_out_norm_g, w_out, norm2_g, w_router_group, b_router_group, w_router_expert, b_router_expert, w_exp_gate, w_exp_up, w_exp_down):
    bsz, s_len, d = x.shape
    n = bsz * s_len
    depth = w_in.shape[0]
    topk = min(INDEX_TOPK, s_len // 4)
    tq = tk = min(256, s_len)
    bh = 4 if bsz % 4 == 0 else 1
    tm_proj = min(256, n)
    tm_out = min(512, n)
    tm_moe = min(1024, n)

    bias = _bias_tiles(rel_bias, tq, tk)
    head_id = jnp.arange(A_WIDTH) // A_HEAD_DIM
    bd = jnp.where(head_id[:, None] == head_id[None, :], 1.0 / A_HEAD_DIM, 0.0).astype(BF16)

    x2 = x.reshape(n, d)
    for l in range(depth):
        w_all = _pack_w_in(w_in[l])
        qg = jnp.tile(q_norm_g[l], A_HEADS)[None, :]
        kg = jnp.tile(k_norm_g[l], A_HEADS)[None, :]
        w2p = jnp.zeros((LANES, B_HEADS * B_KEY_PAD), F32).at[
            _SM_GLR:_SM_GLR + GATE_RANK].set(
                _pad_heads(gla_gate_w2[l], B_HEADS, B_KEY_DIM, B_KEY_PAD)).astype(BF16)
        gbp = _pad_heads(gla_gate_b[l][None, :], B_HEADS, B_KEY_DIM, B_KEY_PAD)

        qa, ka, va, iq, ik, sm, qb, kb, vb, la, rg = _proj(
            x2, norm1_g[l][None, :], w_all, bd, qg, kg, w2p, gbp, tm_proj)

        def b3(a):
            return a.reshape(bsz, s_len, a.shape[-1])

        mask = _select(b3(iq), b3(sm), b3(ik), bh, tq, tk, topk)
        o_a = _attention(b3(qa), b3(ka), b3(va), mask, bias, tq, tk)
        o_b = _gla(b3(qb), b3(kb), b3(vb), b3(la), b3(rg), gla_out_norm_g[l][None, :])

        w_o = w_out[l].astype(BF16)
        wr = jnp.concatenate(
            [w_router_expert[l], w_router_group[l],
             jnp.zeros((d, LANES - N_EXPERTS - N_GROUPS), F32)], axis=1)
        wr_hi = wr.astype(BF16)
        wr_lo = (wr - wr_hi.astype(F32)).astype(BF16)
        br = jnp.concatenate(
            [b_router_expert[l], b_router_group[l],
             jnp.zeros((LANES - N_EXPERTS - N_GROUPS,), F32)])[None, :]
        x1, h2, gates = _outproj(
            x2, o_a.reshape(n, A_WIDTH), o_b.reshape(n, B_WIDTH), w_o[:A_WIDTH], w_o[A_WIDTH:],
            norm2_g[l][None, :], wr_hi, wr_lo, br, tm_out)

        x2 = _moe(h2, gates, x1, w_exp_gate[l].astype(BF16), w_exp_up[l].astype(BF16),
                  w_exp_down[l].astype(BF16), tm_moe)
    return x2.reshape(bsz, s_len, d)
```
